```python
import math
import jax, jax.numpy as jnp
from jax import lax
import numpy as np

D_MODEL = 1024
BATCH = 16
SEQ = 2048
DEPTH = 4

GRID_W = 64
CTX_LEN = 256
N_MIXERS = 2
EPS = 1e-6

HY_SHORT_K = 3
HY_EMB_BANDS = 16
HY_EMB_DIM = 1 + 2 * HY_EMB_BANDS
HY_FILTER_HIDDEN = 64
HY_INNER_MLPS = 2
HY_DECAY_TARGET = 1e-2
HY_FAST_DECAY_PCT = 0.3
HY_SLOW_DECAY_PCT = 1.5

MLA_HEADS = D_MODEL // 128
MLA_NOPE = 128
MLA_ROPE = 64
MLA_QK = MLA_NOPE + MLA_ROPE
MLA_V = 128
MLA_Q_LORA = 384
MLA_KV_LORA = 256
ROPE_BASE = 10000.0
Q_BLOCK = 128
SOFTMAX_SCALE = MLA_QK ** -0.5

N_EXPERTS = 16
N_GROUPS = 4
EXPERTS_PER_GROUP = N_EXPERTS // N_GROUPS
TOPK_GROUPS = 1
TOP_K = 2
D_EXPERT = 512
D_SHARED = 512

kernel_name = "hybrid_hyena_mla_grouped_moe_dit"

F32 = jnp.float32


def rmsnorm(x, g):
    xf = x.astype(F32)
    y = xf * lax.rsqrt(jnp.mean(xf * xf, axis=-1, keepdims=True) + EPS)
    return (y * g.astype(F32)).astype(x.dtype)


def modulate(h, shift, scale):
    return h * (1.0 + scale) + shift


def short_conv3(u, w, b):
    up = jnp.pad(u, ((0, 0), (1, 1), (0, 0)))
    return up[:, :-2] * w[0] + up[:, 1:-1] * w[1] + up[:, 2:] * w[2] + b


def hyena_filter(L, w1, b1, w2, b2, w3, freq):
    t_norm = jnp.linspace(0.0, 1.0, L, dtype=F32)[:, None]
    pos = jnp.arange(L, dtype=F32)[:, None]
    bands = jnp.linspace(1e-4, HY_EMB_BANDS - 1, HY_EMB_BANDS, dtype=F32)[None, :]
    ang = 2.0 * math.pi * bands * pos / L
    z = jnp.concatenate([t_norm, jnp.cos(ang), -jnp.sin(ang)], axis=-1)
    fr = freq.astype(F32)
    h = jnp.sin(fr * (z @ w1.astype(F32) + b1.astype(F32)))
    for m in range(HY_INNER_MLPS):
        h = jnp.sin(fr * (h @ w2[m].astype(F32) + b2[m].astype(F32)))
    h = (h @ w3.astype(F32)).reshape(L, 2, D_MODEL)
    max_decay = math.log(HY_DECAY_TARGET) / HY_FAST_DECAY_PCT
    min_decay = math.log(HY_DECAY_TARGET) / HY_SLOW_DECAY_PCT
    deltas = jnp.linspace(min_decay, max_decay, D_MODEL, dtype=F32)
    decay = jnp.exp(-t_norm * jnp.abs(deltas))
    h = h * decay[:, None, :]
    h_fwd, h_bwd = h[:, 0], h[:, 1]
    k = jnp.concatenate([h_fwd, jnp.zeros((1, D_MODEL), F32), h_bwd[1:][::-1]], axis=0)
    return k / jnp.sum(jnp.abs(k), axis=0, keepdims=True)


def hyena_mixer(u, in_w, in_b, conv_w, conv_b, fw1, fb1, fw2, fb2, fw3, freq, f_bias, out_w, out_b):
    L = u.shape[1]
    z = short_conv3(u @ in_w + in_b, conv_w, conv_b)
    x0, x1, v = jnp.split(z, 3, axis=-1)
    k = hyena_filter(L, fw1, fb1, fw2, fb2, fw3, freq)
    vg = (v * x1).astype(F32)
    n = 2 * L
    y = jnp.fft.irfft(jnp.fft.rfft(vg, n=n, axis=1) * jnp.fft.rfft(k, n=n, axis=0)[None], n=n, axis=1)[:, :L]
    y = y + vg * f_bias.astype(F32)
    y = (x0.astype(F32) * y).astype(u.dtype)
    return y @ out_w + out_b


def axial_rope_angles(L):
    rows = L // GRID_W
    t_row = jnp.repeat(jnp.arange(rows, dtype=F32), GRID_W)
    t_col = jnp.tile(jnp.arange(GRID_W, dtype=F32), rows)
    half = MLA_ROPE // 2
    inv = ROPE_BASE ** (-jnp.arange(0, half, 2, dtype=F32) / half)
    return t_row[:, None] * inv, t_col[:, None] * inv


def rope_1d(x, ang):
    cos = jnp.cos(ang)[None, :, None, :].astype(x.dtype)
    sin = jnp.sin(ang)[None, :, None, :].astype(x.dtype)
    x1, x2 = jnp.split(x, 2, axis=-1)
    return jnp.concatenate([x1 * cos - x2 * sin, x2 * cos + x1 * sin], axis=-1)


def apply_axial_rope(t, angles):
    ang_row, ang_col = angles
    t_nope, t_rope = t[..., :MLA_NOPE], t[..., MLA_NOPE:]
    r_row, r_col = jnp.split(t_rope, 2, axis=-1)
    return jnp.concatenate([t_nope, rope_1d(r_row, ang_row), rope_1d(r_col, ang_col)], axis=-1)


def mla_queries(h, wq_down, q_norm_g, wq_up, q_head_g, angles):
    B, L, _ = h.shape
    q = (rmsnorm(h @ wq_down, q_norm_g) @ wq_up).reshape(B, L, MLA_HEADS, MLA_QK)
    q = rmsnorm(q, q_head_g)
    return q if angles is None else apply_axial_rope(q, angles)


def mla_keys_values(h, wkv_down, kv_norm_g, wkv_up, k_head_g, angles):
    B, L, _ = h.shape
    kv = h @ wkv_down
    c_kv, k_pe = kv[..., :MLA_KV_LORA], kv[..., MLA_KV_LORA:]
    kvu = (rmsnorm(c_kv, kv_norm_g) @ wkv_up).reshape(B, L, MLA_HEADS, MLA_NOPE + MLA_V)
    k_nope, v = kvu[..., :MLA_NOPE], kvu[..., MLA_NOPE:]
    k_pe = jnp.broadcast_to(k_pe[:, :, None, :], (B, L, MLA_HEADS, MLA_ROPE))
    k = rmsnorm(jnp.concatenate([k_nope, k_pe], axis=-1), k_head_g)
    k = k if angles is None else apply_axial_rope(k, angles)
    return k, v


def attend(q, k, v):
    s = jnp.einsum('bqhd,bkhd->bhqk', q, k, preferred_element_type=F32) * SOFTMAX_SCALE
    p = jax.nn.softmax(s, axis=-1).astype(v.dtype)
    return jnp.einsum('bhqk,bkhd->bqhd', p, v)


def mla_mixer(h, hc, wq_down, q_norm_g, wq_up, wkv_down, kv_norm_g, wkv_up, q_head_g, k_head_g, wo, need_ctx):
    B, L, _ = h.shape
    angles = axial_rope_angles(L)
    q = mla_queries(h, wq_down, q_norm_g, wq_up, q_head_g, angles)
    k, v = mla_keys_values(h, wkv_down, kv_norm_g, wkv_up, k_head_g, angles)
    kc, vc = mla_keys_values(hc, wkv_down, kv_norm_g, wkv_up, k_head_g, None)
    k_all = jnp.concatenate([k, kc], axis=1)
    v_all = jnp.concatenate([v, vc], axis=1)
    nb = L // Q_BLOCK
    qb = jnp.moveaxis(q.reshape(B, nb, Q_BLOCK, MLA_HEADS, MLA_QK), 1, 0)
    ob = lax.map(lambda qq: attend(qq, k_all, v_all), qb)
    y = jnp.moveaxis(ob, 0, 1).reshape(B, L, MLA_HEADS * MLA_V) @ wo
    if not need_ctx:
        return y, None
    qc = mla_queries(hc, wq_down, q_norm_g, wq_up, q_head_g, None)
    Bc, Lc, _ = hc.shape
    yc = attend(qc, kc, vc).reshape(Bc, Lc, MLA_HEADS * MLA_V) @ wo
    return y, yc


def swiglu(h, w1, w3, w2):
    return (jax.nn.silu(h @ w1) * (h @ w3)) @ w2


def grouped_moe(h, router_w, router_bias, w1, w3, w2, sw1, sw3, sw2):
    N = h.shape[0]
    s = jax.nn.sigmoid(jnp.dot(h, router_w, preferred_element_type=F32))
    sb = (s + router_bias.astype(F32)).reshape(N, N_GROUPS, EXPERTS_PER_GROUP)
    group_score = jnp.sum(lax.top_k(sb, TOP_K)[0], axis=-1)
    _, g_sel = lax.top_k(group_score, TOPK_GROUPS)
    g_mask = jnp.any(g_sel[:, :, None] == jnp.arange(N_GROUPS)[None, None, :], axis=1)
    sb = jnp.where(g_mask[:, :, None], sb, -jnp.inf).reshape(N, N_EXPERTS)
    _, e_sel = lax.top_k(sb, TOP_K)
    w = jnp.take_along_axis(s, e_sel, axis=-1)
    w = w / jnp.sum(w, axis=-1, keepdims=True)
    gates = jnp.sum(jax.nn.one_hot(e_sel, N_EXPERTS, dtype=F32) * w[..., None], axis=1).astype(h.dtype)
    out = swiglu(h, sw1, sw3, sw2)
    for e in range(N_EXPERTS):
        out = out + gates[:, e:e + 1] * swiglu(h, w1[e], w3[e], w2[e])
    return out


def setup_inputs(seed: int = 0) -> dict:
    key = jax.random.key(seed)
    ks = iter(jax.random.split(key, 48))

    def nrm(shape, scale):
        return scale * jax.random.normal(next(ks), shape, F32)

    def gain(shape):
        return 1.0 + 0.1 * jax.random.normal(next(ks), shape, F32)

    D = D_MODEL
    nh = (DEPTH + 1) // 2
    nm = DEPTH // 2
    H, HID = MLA_HEADS, HY_FILTER_HIDDEN
    return {
        "x": nrm((BATCH, SEQ, D), 1.0),
        "c": nrm((BATCH, D), 1.0),
        "ctx": nrm((BATCH, CTX_LEN, D), 1.0),
        "c_ctx": nrm((D,), 1.0),
        "ada_w": nrm((DEPTH, D, 6 * D), 0.5 * D ** -0.5),
        "ada_b": nrm((DEPTH, 6 * D), 0.02),
        "norm_mix_g": gain((DEPTH, D)),
        "norm_ffn_g": gain((DEPTH, D)),
        "hy_in_w": nrm((nh, D, 3 * D), D ** -0.5),
        "hy_in_b": nrm((nh, 3 * D), 0.02),
        "hy_conv_w": nrm((nh, HY_SHORT_K, 3 * D), HY_SHORT_K ** -0.5),
        "hy_conv_b": nrm((nh, 3 * D), 0.02),
        "hy_f_w1": nrm((nh, HY_EMB_DIM, HID), HY_EMB_DIM ** -0.5),
        "hy_f_b1": nrm((nh, HID), 0.1),
        "hy_f_w2": nrm((nh, HY_INNER_MLPS, HID, HID), HID ** -0.5),
        "hy_f_b2": nrm((nh, HY_INNER_MLPS, HID), 0.1),
        "hy_f_w3": nrm((nh, HID, 2 * D), HID ** -0.5),
        "hy_sin_freq": gain((nh, HID)),
        "hy_f_bias": nrm((nh, D), 0.1),
        "hy_out_w": nrm((nh, D, D), D ** -0.5),
        "hy_out_b": nrm((nh, D), 0.02),
        "mla_wq_down": nrm((nm, D, MLA_Q_LORA), D ** -0.5),
        "mla_q_norm_g": gain((nm, MLA_Q_LORA)),
        "mla_wq_up": nrm((nm, MLA_Q_LORA, H * MLA_QK), MLA_Q_LORA ** -0.5),
        "mla_wkv_down": nrm((nm, D, MLA_KV_LORA + MLA_ROPE), D ** -0.5),
        "mla_kv_norm_g": gain((nm, MLA_KV_LORA)),
        "mla_wkv_up": nrm((nm, MLA_KV_LORA, H * (MLA_NOPE + MLA_V)), MLA_KV_LORA ** -0.5),
        "mla_q_head_g": gain((nm, MLA_QK)),
        "mla_k_head_g": gain((nm, MLA_QK)),
        "mla_wo": nrm((nm, H * MLA_V, D), (H * MLA_V) ** -0.5),
        "router_w": nrm((D, N_EXPERTS), D ** -0.5),
        "router_bias": nrm((N_EXPERTS,), 0.01),
        "exp_w1": nrm((DEPTH, N_EXPERTS, D, D_EXPERT), D ** -0.5),
        "exp_w3": nrm((DEPTH, N_EXPERTS, D, D_EXPERT), D ** -0.5),
        "exp_w2": nrm((DEPTH, N_EXPERTS, D_EXPERT, D), D_EXPERT ** -0.5),
        "sh_w1": nrm((DEPTH, D, D_SHARED), D ** -0.5),
        "sh_w3": nrm((DEPTH, D, D_SHARED), D ** -0.5),
        "sh_w2": nrm((DEPTH, D_SHARED, D), D_SHARED ** -0.5),
    }


def reference(x, c, ctx, c_ctx, ada_w, ada_b, norm_mix_g, norm_ffn_g,
              hy_in_w, hy_in_b, hy_conv_w, hy_conv_b, hy_f_w1, hy_f_b1, hy_f_w2, hy_f_b2, hy_f_w3,
              hy_sin_freq, hy_f_bias, hy_out_w, hy_out_b,
              mla_wq_down, mla_q_norm_g, mla_wq_up, mla_wkv_down, mla_kv_norm_g, mla_wkv_up,
              mla_q_head_g, mla_k_head_g, mla_wo,
              router_w, router_bias, exp_w1, exp_w3, exp_w2, sh_w1, sh_w3, sh_w2):
    D = D_MODEL
    xc = ctx
    silu_c = jax.nn.silu(c)
    silu_cc = jax.nn.silu(c_ctx)
    for i in range(DEPTH):
        need_ctx = i < DEPTH - 1
        j = i // N_MIXERS
        mod = (silu_c @ ada_w[i] + ada_b[i])[:, None, :]
        mod_c = silu_cc @ ada_w[i] + ada_b[i]
        sh1, sc1, g1, sh2, sc2, g2 = jnp.split(mod, 6, axis=-1)
        sh1c, sc1c, g1c, sh2c, sc2c, g2c = jnp.split(mod_c, 6, axis=-1)
        h = modulate(rmsnorm(x, norm_mix_g[i]), sh1, sc1)
        hc = modulate(rmsnorm(xc, norm_mix_g[i]), sh1c, sc1c)
        if i % N_MIXERS == 0:
            hp = (hy_in_w[j], hy_in_b[j], hy_conv_w[j], hy_conv_b[j], hy_f_w1[j], hy_f_b1[j],
                  hy_f_w2[j], hy_f_b2[j], hy_f_w3[j], hy_sin_freq[j], hy_f_bias[j], hy_out_w[j], hy_out_b[j])
            y = hyena_mixer(h, *hp)
            yc = hyena_mixer(hc, *hp) if need_ctx else None
        else:
            y, yc = mla_mixer(h, hc, mla_wq_down[j], mla_q_norm_g[j], mla_wq_up[j], mla_wkv_down[j],
                              mla_kv_norm_g[j], mla_wkv_up[j], mla_q_head_g[j], mla_k_head_g[j], mla_wo[j],
                              need_ctx)
        x = x + g1 * y
        h2 = modulate(rmsnorm(x, norm_ffn_g[i]), sh2, sc2)
        ep = (router_w, router_bias, exp_w1[i], exp_w3[i], exp_w2[i], sh_w1[i], sh_w3[i], sh_w2[i])
        if need_ctx:
            xc = xc + g1c * yc
            h2c = modulate(rmsnorm(xc, norm_ffn_g[i]), sh2c, sc2c)
            n_lat = h2.shape[0] * h2.shape[1]
            f = grouped_moe(jnp.concatenate([h2.reshape(-1, D), h2c.reshape(-1, D)], axis=0), *ep)
            x = x + g2 * f[:n_lat].reshape(x.shape)
            xc = xc + g2c * f[n_lat:].reshape(xc.shape)
        else:
            x = x + g2 * grouped_moe(h2.reshape(-1, D), *ep).reshape(x.shape)
    return x
```

```python
import functools
import math

import numpy as np
import jax
import jax.numpy as jnp
from jax import lax
from jax.experimental import pallas as pl
from jax.experimental.pallas import tpu as pltpu

F32 = jnp.float32
BF16 = jnp.bfloat16

DEPTH = 4
GRID_W = 64
EPS = 1e-6

HY_EMB_BANDS = 16
HY_EMB_DIM = 1 + 2 * HY_EMB_BANDS
HY_HID = 64
HY_INNER_MLPS = 2
HY_DECAY_TARGET = 1e-2
HY_FAST_DECAY_PCT = 0.3
HY_SLOW_DECAY_PCT = 1.5

MLA_HEADS = 8
MLA_NOPE = 128
MLA_ROPE = 64
MLA_QK = MLA_NOPE + MLA_ROPE
MLA_V = 128
MLA_Q_LORA = 384
MLA_KV_LORA = 256
ROPE_BASE = 10000.0
SOFTMAX_SCALE = MLA_QK ** -0.5
HEAD_PAD = 256

N_EXPERTS = 16
N_GROUPS = 4
EXPERTS_PER_GROUP = 4
N_PAIRS = 6
N_CLASSES = N_GROUPS * N_PAIRS
PAIR_LO = (0, 0, 0, 1, 1, 2)
PAIR_HI = (1, 2, 3, 2, 3, 3)
D_EXPERT = 512

LANES = 128
MOD_ROWS = 32
MOE_TILE = 256
CH_TILE = 256
VMEM_LIMIT = 56 * 1024 * 1024


def _cparams(*sem):
    return pltpu.CompilerParams(dimension_semantics=sem, vmem_limit_bytes=VMEM_LIMIT)


def _token_tile(L):
    return 512 if L % 512 == 0 else 256


def _sigmoid(v):
    return 1.0 / (1.0 + jnp.exp(-v))


def _norm_mod(x, g, shift, scale):
    ms = jnp.mean(x * x, axis=-1, keepdims=True)
    return (x * lax.rsqrt(ms + EPS) * g) * (1.0 + scale) + shift


def _split_bf16(v):
    hi = v.astype(BF16)
    lo = (v - hi.astype(F32)).astype(BF16)
    return hi, lo


def _dot(a, b):
    return jnp.dot(a, b, preferred_element_type=F32)


def _dot_nt(a, b):
    return lax.dot_general(a, b, (((1,), (1,)), ((), ())), preferred_element_type=F32)


def _ada_kernel(cc_ref, w_ref, b_ref, o_ref):
    cc = cc_ref[...]
    s = cc * _sigmoid(cc)
    s_hi, s_lo = _split_bf16(s)
    w_hi, w_lo = _split_bf16(w_ref[0])
    o_ref[0] = _dot(s_hi, w_hi) + _dot(s_hi, w_lo) + _dot(s_lo, w_hi) + b_ref[0]


def _ada_mods(cc, ada_w, ada_b):
    depth, d, d6 = ada_w.shape
    tn = 1536
    out = pl.pallas_call(
        _ada_kernel,
        grid=(depth, d6 // tn),
        in_specs=[
            pl.BlockSpec((MOD_ROWS, d), lambda i, j: (0, 0)),
            pl.BlockSpec((1, d, tn), lambda i, j: (i, 0, j)),
            pl.BlockSpec((1, 1, tn), lambda i, j: (i, 0, j)),
        ],
        out_specs=pl.BlockSpec((1, MOD_ROWS, tn), lambda i, j: (i, 0, j)),
        out_shape=jax.ShapeDtypeStruct((depth, MOD_ROWS, d6), F32),
        compiler_params=_cparams("arbitrary", "arbitrary"),
        name="ada_mods",
    )(cc, ada_w, ada_b.reshape(depth, 1, d6))
    return out.reshape(depth, MOD_ROWS, 6, 1, d).transpose(0, 2, 1, 3, 4)


class _Tokens:
    def __init__(self, B, L, LC, D):
        self.B, self.L, self.LC, self.D = B, L, LC, D
        self.n_lat = B * L
        self.n_ctx = B * LC
        self.n_tok = self.n_lat + self.n_ctx
        self.tm = _token_tile(L)
        assert L % self.tm == 0 and self.n_ctx % self.tm == 0 and L % LC == 0
        self.nt_lat = self.n_lat // self.tm
        self.nt_all = self.n_tok // self.tm

    def mod_spec(self, kind):
        tm, L, n_lat, B = self.tm, self.L, self.n_lat, self.B

        def index(t):
            row = jnp.where(t * tm < n_lat, (t * tm) // L, B)
            return (kind, row, 0, 0)

        return pl.BlockSpec((None, None, 1, self.D), index)


def _row_spec(width):
    return pl.BlockSpec((1, width), lambda t: (0, 0))


def _hy_in_kernel(x_ref, g_ref, sh_ref, sc_ref, w_ref, b_ref, o_ref):
    h = _norm_mod(x_ref[...], g_ref[...], sh_ref[...], sc_ref[...]).astype(BF16)
    o_ref[...] = (_dot(h, w_ref[...]) + b_ref[...]).astype(BF16)


def _hy_in(tk, X, mod, g, w_bf16, b):
    D, tm = tk.D, tk.tm
    n_out = w_bf16.shape[1]
    return pl.pallas_call(
        _hy_in_kernel,
        grid=(tk.nt_all,),
        in_specs=[
            pl.BlockSpec((tm, D), lambda t: (t, 0)),
            _row_spec(D),
            tk.mod_spec(0),
            tk.mod_spec(1),
            pl.BlockSpec((D, n_out), lambda t: (0, 0)),
            _row_spec(n_out),
        ],
        out_specs=pl.BlockSpec((tm, n_out), lambda t: (t, 0)),
        out_shape=jax.ShapeDtypeStruct((tk.n_tok, n_out), BF16),
        compiler_params=_cparams("parallel"),
        name="hy_in_proj",
    )(X, g.reshape(1, D), mod, mod, w_bf16, b.reshape(1, n_out))


@functools.lru_cache(maxsize=None)
def _dft_tables(L):
    idx = np.arange(L, dtype=np.int64)
    prod = np.mod(np.outer(idx, idx), 2 * L).astype(np.float64)
    ang = prod * (math.pi / L)
    return np.cos(ang).astype(np.float32), np.sin(ang).astype(np.float32)


@functools.lru_cache(maxsize=None)
def _filter_tables(L, D):
    t_norm = np.linspace(0.0, 1.0, L, dtype=np.float64)[:, None]
    pos = np.arange(L, dtype=np.float64)[:, None]
    bands = np.linspace(1e-4, HY_EMB_BANDS - 1, HY_EMB_BANDS, dtype=np.float64)[None, :]
    ang = 2.0 * math.pi * bands * pos / L
    z = np.concatenate([t_norm, np.cos(ang), -np.sin(ang)], axis=-1)
    z = np.pad(z, ((0, 0), (0, HY_HID - HY_EMB_DIM)))
    max_decay = math.log(HY_DECAY_TARGET) / HY_FAST_DECAY_PCT
    min_decay = math.log(HY_DECAY_TARGET) / HY_SLOW_DECAY_PCT
    deltas = np.linspace(min_decay, max_decay, D, dtype=np.float64)
    decay = np.exp(-t_norm * np.abs(deltas))
    rev = (L - np.arange(L)) % L
    z_rev = z[rev]
    decay_rev = decay[rev].copy()
    decay_rev[0] = 0.0
    f32 = lambda a: np.asarray(a, dtype=np.float32)
    return f32(z), f32(z_rev), f32(decay), f32(decay_rev)


def _hy_filter_kernel(z_ref, zr_ref, dec_ref, decr_ref, w1_ref, b1_ref, w2_ref, b2_ref, w3a_ref, w3b_ref,
                      fr_ref, c_ref, s_ref, kr_ref, ki_ref, kn_ref):
    hp = lax.Precision.HIGHEST
    fr = fr_ref[...]

    def dot_hp(a, b):
        return jnp.dot(a, b, precision=hp, preferred_element_type=F32)

    def mlp(z):
        h = jnp.sin(fr * (dot_hp(z, w1_ref[...]) + b1_ref[...]))
        for m in range(HY_INNER_MLPS):
            h = jnp.sin(fr * (dot_hp(h, w2_ref[m]) + b2_ref[m]))
        return h

    ka = dot_hp(mlp(z_ref[...]), w3a_ref[...]) * dec_ref[...]
    kb = dot_hp(mlp(zr_ref[...]), w3b_ref[...]) * decr_ref[...]
    norm = jnp.sum(jnp.abs(ka), axis=0, keepdims=True) + jnp.sum(jnp.abs(kb), axis=0, keepdims=True)
    inv = 1.0 / norm
    ka = ka * inv
    kb = kb * inv
    L = ka.shape[0]
    inv_n = 1.0 / (2 * L)
    row = lax.broadcasted_iota(jnp.int32, ka.shape, 0)
    alt = jnp.where((row & 1) == 0, 1.0, -1.0)
    kn_ref[...] = jnp.sum(alt * (ka + kb), axis=0, keepdims=True) * inv_n
    ka_b = ka.astype(BF16)
    kb_b = kb.astype(BF16)
    cm = c_ref[...]
    sm = s_ref[...]
    k_cos = _dot(cm, ka_b) + alt * _dot(cm, kb_b)
    k_sin = _dot(sm, ka_b) + alt * _dot(sm, kb_b)
    a = jnp.where(row == 0, inv_n, 2.0 * inv_n)
    kr_ref[...] = a * k_cos
    ki_ref[...] = -(a * k_sin)


def _hy_filter(L, D, fw1, fb1, fw2, fb2, fw3, freq, cm, sm):
    z, z_rev, decay, decay_rev = _filter_tables(L, D)
    ct = CH_TILE
    nj = D // ct
    w1p = jnp.pad(fw1, ((0, HY_HID - HY_EMB_DIM), (0, 0)))
    full = lambda shape: pl.BlockSpec(shape, lambda j: (0,) * len(shape))
    single = pl.Buffered(1)
    return pl.pallas_call(
        _hy_filter_kernel,
        grid=(nj,),
        in_specs=[
            full((L, HY_HID)),
            full((L, HY_HID)),
            pl.BlockSpec((L, ct), lambda j: (0, j)),
            pl.BlockSpec((L, ct), lambda j: (0, j)),
            full((HY_HID, HY_HID)),
            full((1, HY_HID)),
            full((HY_INNER_MLPS, HY_HID, HY_HID)),
            full((HY_INNER_MLPS, 1, HY_HID)),
            pl.BlockSpec((HY_HID, ct), lambda j: (0, j)),
            pl.BlockSpec((HY_HID, ct), lambda j: (0, nj + j)),
            full((1, HY_HID)),
            pl.BlockSpec((L, L), lambda j: (0, 0), pipeline_mode=single),
            pl.BlockSpec((L, L), lambda j: (0, 0), pipeline_mode=single),
        ],
        out_specs=[
            pl.BlockSpec((L, ct), lambda j: (0, j)),
            pl.BlockSpec((L, ct), lambda j: (0, j)),
            pl.BlockSpec((1, ct), lambda j: (0, j)),
        ],
        out_shape=[
            jax.ShapeDtypeStruct((L, D), F32),
            jax.ShapeDtypeStruct((L, D), F32),
            jax.ShapeDtypeStruct((1, D), F32),
        ],
        compiler_params=_cparams("arbitrary"),
        name=f"hy_filter_{L}",
    )(jnp.asarray(z), jnp.asarray(z_rev), jnp.asarray(decay), jnp.asarray(decay_rev), w1p,
      fb1.reshape(1, HY_HID), fw2, fb2.reshape(HY_INNER_MLPS, 1, HY_HID), fw3, fw3,
      freq.reshape(1, HY_HID), cm, sm)


def _hy_conv_kernel(z0_ref, z1_ref, zv_ref, cw0_ref, cw1_ref, cwv_ref, cb0_ref, cb1_ref, cbv_ref, fb_ref,
                    kr_ref, ki_ref, kn_ref, c_ref, s_ref, o_ref):
    L = z0_ref.shape[0]
    row = lax.broadcasted_iota(jnp.int32, z0_ref.shape, 0)
    first = row == 0
    last = row == L - 1

    def conv3(z_ref, w_ref, b_ref):
        z = z_ref[...].astype(F32)
        w = w_ref[...]
        prev = jnp.where(first, 0.0, pltpu.roll(z, 1, 0))
        nxt = jnp.where(last, 0.0, pltpu.roll(z, L - 1, 0))
        return prev * w[0:1] + z * w[1:2] + nxt * w[2:3] + b_ref[...]

    vg = conv3(zv_ref, cwv_ref, cbv_ref) * conv3(z1_ref, cw1_ref, cb1_ref)
    alt = jnp.where((row & 1) == 0, 1.0, -1.0)
    x_nyq = jnp.sum(alt * vg, axis=0, keepdims=True)
    vb = vg.astype(BF16)
    cm = c_ref[...]
    sm = s_ref[...]
    xc = _dot(cm, vb)
    xs = _dot(sm, vb)
    kr = kr_ref[...]
    ki = ki_ref[...]
    u = (xc * kr + xs * ki).astype(BF16)
    w = (xs * kr - xc * ki).astype(BF16)
    y = _dot(cm, u) + _dot(sm, w) + alt * (x_nyq * kn_ref[...])
    y = y + vg * fb_ref[...]
    o_ref[...] = (conv3(z0_ref, cw0_ref, cb0_ref) * y).astype(BF16)


def _hy_conv(tk, zpre, conv_w, conv_b, f_bias, kr, ki, kn, cm, sm, L, row_block0, prev_out):
    D, B = tk.D, tk.B
    ct = CH_TILE
    nj = D // ct
    single = pl.Buffered(1)
    zspec = lambda part: pl.BlockSpec((L, ct), lambda j, b: (row_block0 + b, part * nj + j))
    wspec = lambda part: pl.BlockSpec((3, ct), lambda j, b: (0, part * nj + j))
    bspec = lambda part: pl.BlockSpec((1, ct), lambda j, b: (0, part * nj + j))
    in_specs = [
        zspec(0), zspec(1), zspec(2),
        wspec(0), wspec(1), wspec(2),
        bspec(0), bspec(1), bspec(2),
        pl.BlockSpec((1, ct), lambda j, b: (0, j)),
        pl.BlockSpec((L, ct), lambda j, b: (0, j), pipeline_mode=single),
        pl.BlockSpec((L, ct), lambda j, b: (0, j), pipeline_mode=single),
        pl.BlockSpec((1, ct), lambda j, b: (0, j)),
        pl.BlockSpec((L, L), lambda j, b: (0, 0), pipeline_mode=single),
        pl.BlockSpec((L, L), lambda j, b: (0, 0), pipeline_mode=single),
    ]
    args = [zpre, zpre, zpre, conv_w, conv_w, conv_w, conv_b, conv_b, conv_b, f_bias, kr, ki, kn, cm, sm]
    kern = _hy_conv_kernel
    aliases = {}
    if prev_out is not None:
        in_specs.append(pl.BlockSpec(memory_space=pl.ANY))
        args.append(prev_out)
        aliases = {len(args) - 1: 0}
        kern = lambda *refs: _hy_conv_kernel(*refs[:15], refs[16])
    return pl.pallas_call(
        kern,
        grid=(nj, B),
        in_specs=in_specs,
        out_specs=pl.BlockSpec((L, ct), lambda j, b: (row_block0 + b, j)),
        out_shape=jax.ShapeDtypeStruct((tk.n_tok, D), BF16),
        input_output_aliases=aliases,
        compiler_params=_cparams("arbitrary", "arbitrary"),
        name=f"hy_conv_{L}",
    )(*args)


def _route_rows(s, sb):
    srow = [s[e:e + 1, :] for e in range(N_EXPERTS)]
    brow = [sb[e:e + 1, :] for e in range(N_EXPERTS)]
    gscore = []
    for g in range(N_GROUPS):
        a, b, c, d = brow[4 * g:4 * g + 4]
        m = jnp.maximum(jnp.maximum(a + b, a + c), jnp.maximum(a + d, b + c))
        gscore.append(jnp.maximum(m, jnp.maximum(b + d, c + d)))
    best = gscore[0]
    gi = jnp.zeros_like(best)
    for g in range(1, N_GROUPS):
        upd = gscore[g] > best
        best = jnp.where(upd, gscore[g], best)
        gi = jnp.where(upd, float(g), gi)

    def pick(rows, j):
        return jnp.where(gi == 0.0, rows[j],
                         jnp.where(gi == 1.0, rows[4 + j], jnp.where(gi == 2.0, rows[8 + j], rows[12 + j])))

    v = [pick(brow, j) for j in range(4)]
    u = [pick(srow, j) for j in range(4)]
    sel = []
    for j in range(4):
        cnt = jnp.zeros_like(best)
        for k in range(4):
            if k == j:
                continue
            beats = (v[k] >= v[j]) if k < j else (v[k] > v[j])
            cnt = cnt + jnp.where(beats, 1.0, 0.0)
        sel.append(cnt < 2.0)
    lo = jnp.where(sel[0], 0.0, jnp.where(sel[1], 1.0, 2.0))
    hi = jnp.where(sel[3], 3.0, jnp.where(sel[2], 2.0, 1.0))
    w_lo = jnp.where(sel[0], u[0], jnp.where(sel[1], u[1], u[2]))
    w_hi = jnp.where(sel[3], u[3], jnp.where(sel[2], u[2], u[1]))
    pair = lo * (7.0 - lo) * 0.5 + (hi - lo - 1.0)
    tot = w_lo + w_hi
    return gi * float(N_PAIRS) + pair, w_lo / tot, w_hi / tot


def _post_kernel(y_ref, w_ref, b_ref, x_ref, g1_ref, ng_ref, sh_ref, sc_ref, rw_ref, rb_ref,
                 xo_ref, h_ref, r_ref):
    mix = _dot(y_ref[...], w_ref[...]) + b_ref[...]
    xn = x_ref[...] + g1_ref[...] * mix
    xo_ref[...] = xn
    h2 = _norm_mod(xn, ng_ref[...], sh_ref[...], sc_ref[...]).astype(BF16)
    h_ref[...] = h2
    s = _sigmoid(_dot_nt(rw_ref[...], h2))
    cls, w_lo, w_hi = _route_rows(s, s + rb_ref[...])
    r_ref[0, 0:1, :] = cls
    r_ref[0, 1:2, :] = w_lo
    r_ref[0, 2:3, :] = w_hi
    r_ref[0, 3:8, :] = jnp.zeros((5, cls.shape[1]), F32)


def _post_mixer(tk, n_tiles, Y, w_bf16, b, X, mod, ng, rw_t, rb):
    D, tm = tk.D, tk.tm
    K = w_bf16.shape[0]
    n_rows = n_tiles * tm
    return pl.pallas_call(
        _post_kernel,
        grid=(n_tiles,),
        in_specs=[
            pl.BlockSpec((tm, K), lambda t: (t, 0)),
            pl.BlockSpec((K, D), lambda t: (0, 0)),
            _row_spec(D),
            pl.BlockSpec((tm, D), lambda t: (t, 0)),
            tk.mod_spec(2),
            _row_spec(D),
            tk.mod_spec(3),
            tk.mod_spec(4),
            pl.BlockSpec((N_EXPERTS, D), lambda t: (0, 0)),
            pl.BlockSpec((N_EXPERTS, 1), lambda t: (0, 0)),
        ],
        out_specs=[
            pl.BlockSpec((tm, D), lambda t: (t, 0)),
            pl.BlockSpec((tm, D), lambda t: (t, 0)),
            pl.BlockSpec((1, 8, tm), lambda t: (t, 0, 0)),
        ],
        out_shape=[
            jax.ShapeDtypeStruct((n_rows, D), F32),
            jax.ShapeDtypeStruct((n_rows, D), BF16),
            jax.ShapeDtypeStruct((n_tiles, 8, tm), F32),
        ],
        compiler_params=_cparams("parallel"),
        name="post_mixer",
    )(Y, w_bf16, b.reshape(1, D), X, mod, ng.reshape(1, D), mod, mod, rw_t, rb)


def _moe_kernel(ea_ref, eb_ref, nu_ref, h_ref, g_ref, a13_ref, a2_ref, b13_ref, b2_ref, s13_ref, s2_ref, o_ref):
    t = pl.program_id(0)

    @pl.when(t < nu_ref[0])
    def _():
        h = h_ref[...]
        g = g_ref[...]

        def ffn(w13_ref, w2_ref, gate):
            a = _dot(h, w13_ref[...])
            de = a.shape[1] // 2
            a1 = a[:, :de]
            hid = a1 * _sigmoid(a1) * a[:, de:]
            if gate is not None:
                hid = hid * gate
            return _dot(hid.astype(BF16), w2_ref[...])

        out = ffn(s13_ref, s2_ref, None) + ffn(a13_ref, a2_ref, g[:, 0:1]) + ffn(b13_ref, b2_ref, g[:, 1:2])
        o_ref[...] = out.astype(BF16)

    @pl.when(t >= nu_ref[0])
    def _():
        o_ref[...] = jnp.zeros(o_ref.shape, BF16)


def _moe(hs, gates, tile_ea, tile_eb, n_used, w13, w2, s13, s2):
    ns, D = hs.shape
    tmm = MOE_TILE
    n_tiles = ns // tmm
    de2 = w13.shape[2]
    de = w2.shape[1]
    grid_spec = pltpu.PrefetchScalarGridSpec(
        num_scalar_prefetch=3,
        grid=(n_tiles,),
        in_specs=[
            pl.BlockSpec((tmm, D), lambda t, ea, eb, nu: (t, 0)),
            pl.BlockSpec((tmm, 2), lambda t, ea, eb, nu: (t, 0)),
            pl.BlockSpec((None, D, de2), lambda t, ea, eb, nu: (ea[t], 0, 0)),
            pl.BlockSpec((None, de, D), lambda t, ea, eb, nu: (ea[t], 0, 0)),
            pl.BlockSpec((None, D, de2), lambda t, ea, eb, nu: (eb[t], 0, 0)),
            pl.BlockSpec((None, de, D), lambda t, ea, eb, nu: (eb[t], 0, 0)),
            pl.BlockSpec(s13.shape, lambda t, ea, eb, nu: (0, 0)),
            pl.BlockSpec(s2.shape, lambda t, ea, eb, nu: (0, 0)),
        ],
        out_specs=pl.BlockSpec((tmm, D), lambda t, ea, eb, nu: (t, 0)),
    )
    return pl.pallas_call(
        _moe_kernel,
        grid_spec=grid_spec,
        out_shape=jax.ShapeDtypeStruct((ns, D), BF16),
        compiler_params=_cparams("arbitrary"),
        name="moe_pairs",
    )(tile_ea, tile_eb, n_used, hs, gates, w13, w2, w13, w2, s13, s2)


def _moe_layer(h2, route, w13, w2, s13, s2):
    n, D = h2.shape
    tmm = MOE_TILE
    cls = route[:, 0, :].reshape(n).astype(jnp.int32)
    w_lo = route[:, 1, :].reshape(n)
    w_hi = route[:, 2, :].reshape(n)
    onehot = (cls[:, None] == jnp.arange(N_CLASSES, dtype=jnp.int32)[None, :]).astype(jnp.int32)
    csum = jnp.cumsum(onehot, axis=0)
    rank = jnp.sum(onehot * csum, axis=1) - 1
    counts = csum[-1]
    padded = ((counts + tmm - 1) // tmm) * tmm
    ends = jnp.cumsum(padded)
    starts = ends - padded
    pos = starts[cls] + rank
    n_tiles = (n + N_CLASSES * (tmm - 1)) // tmm + 1
    ns = n_tiles * tmm
    inv = jnp.zeros((ns,), jnp.int32).at[pos].set(jnp.arange(n, dtype=jnp.int32))
    last_cls = jnp.max(jnp.where(counts > 0, jnp.arange(N_CLASSES, dtype=jnp.int32), 0))
    tile_start = jnp.arange(n_tiles, dtype=jnp.int32) * tmm
    tile_cls = jnp.minimum(jnp.searchsorted(ends, tile_start, side="right").astype(jnp.int32), last_cls)
    grp = tile_cls // N_PAIRS
    pair = tile_cls % N_PAIRS
    tile_ea = grp * EXPERTS_PER_GROUP + jnp.asarray(PAIR_LO, jnp.int32)[pair]
    tile_eb = grp * EXPERTS_PER_GROUP + jnp.asarray(PAIR_HI, jnp.int32)[pair]
    n_used = (ends[-1] // tmm).astype(jnp.int32).reshape(1)
    hs = jnp.take(h2, inv, axis=0)
    gates = jnp.take(jnp.stack([w_lo, w_hi], axis=1), inv, axis=0)
    fs = _moe(hs, gates, tile_ea, tile_eb, n_used, w13, w2, s13, s2)
    return jnp.take(fs, pos, axis=0)


def _rope_swap_perm():
    q = MLA_ROPE // 4
    p = np.arange(MLA_ROPE)
    return np.where((p % (2 * q)) < q, p + q, p - q)


@functools.lru_cache(maxsize=None)
def _rope_tables(L, tm):
    rows = L // GRID_W
    t_row = np.repeat(np.arange(rows, dtype=np.float64), GRID_W)
    t_col = np.tile(np.arange(GRID_W, dtype=np.float64), rows)
    half = MLA_ROPE // 2
    q = half // 2
    inv = (ROPE_BASE ** (-np.arange(0, half, 2, dtype=np.float32) / np.float32(half))).astype(np.float64)
    ang = np.concatenate([t_row[:, None] * inv, t_row[:, None] * inv,
                          t_col[:, None] * inv, t_col[:, None] * inv], axis=1)
    sign = np.tile(np.concatenate([-np.ones(q), np.ones(q)]), 2)[None, :]
    cos = np.zeros((L + tm, LANES), np.float32)
    sin = np.zeros((L + tm, LANES), np.float32)
    cos[:L, :MLA_ROPE] = np.cos(ang)
    sin[:L, :MLA_ROPE] = np.sin(ang) * sign
    cos[L:, :MLA_ROPE] = 1.0
    return cos, sin


def _mla_proj_kernel(x_ref, g_ref, sh_ref, sc_ref, wqd_ref, qng_ref, wqu_ref, wkvd_ref, kvng_ref, wkvu_ref,
                     qg_ref, kg_ref, cos_ref, sin_ref, q_ref, k_ref, v_ref):
    h = _norm_mod(x_ref[...], g_ref[...], sh_ref[...], sc_ref[...]).astype(BF16)
    cos = cos_ref[...]
    sin = sin_ref[...]
    nq = MLA_HEADS * HEAD_PAD

    def rms(v, g):
        return v * lax.rsqrt(jnp.mean(v * v, axis=-1, keepdims=True) + EPS) * g

    q_lat = rms(_dot(h, wqd_ref[...]), qng_ref[...]).astype(BF16)
    q_all = _dot(q_lat, wqu_ref[...])
    qg = qg_ref[...]
    g_nope, gc, gs = qg[0:1], cos * qg[1:2], sin * qg[2:3]
    for hd in range(MLA_HEADS):
        nope = q_all[:, hd * HEAD_PAD:hd * HEAD_PAD + LANES]
        rope = q_all[:, hd * HEAD_PAD + LANES:(hd + 1) * HEAD_PAD]
        part = q_all[:, nq + hd * LANES:nq + (hd + 1) * LANES]
        ssq = jnp.sum(nope * nope, axis=-1, keepdims=True) + jnp.sum(rope * rope, axis=-1, keepdims=True)
        r = lax.rsqrt(ssq * (1.0 / MLA_QK) + EPS) * SOFTMAX_SCALE
        q_ref[:, hd * HEAD_PAD:hd * HEAD_PAD + LANES] = (nope * r * g_nope).astype(BF16)
        q_ref[:, hd * HEAD_PAD + LANES:(hd + 1) * HEAD_PAD] = (r * (rope * gc + part * gs)).astype(BF16)

    kv = _dot(h, wkvd_ref[...])
    c_kv = rms(kv[:, :MLA_KV_LORA], kvng_ref[...]).astype(BF16)
    k_pe = kv[:, MLA_KV_LORA:MLA_KV_LORA + LANES]
    k_pe_part = kv[:, MLA_KV_LORA + LANES:MLA_KV_LORA + 2 * LANES]
    kvu = _dot(c_kv, wkvu_ref[...])
    kg = kg_ref[...]
    gk_nope, gkc, gks = kg[0:1], cos * kg[1:2], sin * kg[2:3]
    pe_ssq = jnp.sum(k_pe * k_pe, axis=-1, keepdims=True)
    pe_rot = k_pe * gkc + k_pe_part * gks
    for hd in range(MLA_HEADS):
        nope = kvu[:, hd * LANES:(hd + 1) * LANES]
        ssq = jnp.sum(nope * nope, axis=-1, keepdims=True) + pe_ssq
        r = lax.rsqrt(ssq * (1.0 / MLA_QK) + EPS)
        k_ref[:, hd * HEAD_PAD:hd * HEAD_PAD + LANES] = (nope * r * gk_nope).astype(BF16)
        k_ref[:, hd * HEAD_PAD + LANES:(hd + 1) * HEAD_PAD] = (r * pe_rot).astype(BF16)
    v_ref[...] = kvu[:, MLA_HEADS * LANES:].astype(BF16)


def _head_gain_rows(g):
    perm = _rope_swap_perm()
    pad = lambda v: jnp.pad(v, (0, LANES - MLA_ROPE))
    rows = jnp.stack([g[:MLA_NOPE], pad(g[MLA_NOPE:]), pad(g[MLA_NOPE:][perm])])
    return jnp.pad(rows, ((0, 8 - 3), (0, 0)))


def _mla_weights(wq_up, wkv_down, wkv_up):
    perm = _rope_swap_perm()
    H = MLA_HEADS
    r = wq_up.shape[0]
    wq = wq_up.reshape(r, H, MLA_QK)
    nope, rope = wq[:, :, :MLA_NOPE], wq[:, :, MLA_NOPE:]
    zeros = jnp.zeros((r, H, LANES - MLA_ROPE), wq_up.dtype)
    main = jnp.concatenate([nope, rope, zeros], axis=-1).reshape(r, H * HEAD_PAD)
    part = jnp.concatenate([rope[:, :, perm], zeros], axis=-1).reshape(r, H * LANES)
    wqu = jnp.concatenate([main, part], axis=1).astype(BF16)
    d = wkv_down.shape[0]
    c, pe = wkv_down[:, :MLA_KV_LORA], wkv_down[:, MLA_KV_LORA:]
    z = jnp.zeros((d, LANES - MLA_ROPE), wkv_down.dtype)
    wkvd = jnp.concatenate([c, pe, z, pe[:, perm], z], axis=1).astype(BF16)
    wu = wkv_up.reshape(wkv_up.shape[0], H, MLA_NOPE + MLA_V)
    wkvu = jnp.concatenate([wu[:, :, :MLA_NOPE].reshape(-1, H * MLA_NOPE),
                            wu[:, :, MLA_NOPE:].reshape(-1, H * MLA_V)], axis=1).astype(BF16)
    return wqu, wkvd, wkvu


def _mla_proj(tk, X, mod, g, wqd, qng, wqu, wkvd, kvng, wkvu, qg, kg):
    D, tm, L = tk.D, tk.tm, tk.L
    cos, sin = _rope_tables(L, tm)
    n_lat, per_seq = tk.n_lat, L // tm

    def table_index(t):
        return (jnp.where(t * tm < n_lat, t % per_seq, per_seq), 0)

    const = lambda a: pl.BlockSpec(a.shape, lambda t: (0,) * a.ndim)
    qw = MLA_HEADS * HEAD_PAD
    vw = MLA_HEADS * MLA_V
    return pl.pallas_call(
        _mla_proj_kernel,
        grid=(tk.nt_all,),
        in_specs=[
            pl.BlockSpec((tm, D), lambda t: (t, 0)),
            _row_spec(D),
            tk.mod_spec(0),
            tk.mod_spec(1),
            const(wqd), _row_spec(MLA_Q_LORA), const(wqu), const(wkvd), _row_spec(MLA_KV_LORA), const(wkvu),
            const(qg), const(kg),
            pl.BlockSpec((tm, LANES), table_index),
            pl.BlockSpec((tm, LANES), table_index),
        ],
        out_specs=[
            pl.BlockSpec((tm, qw), lambda t: (t, 0)),
            pl.BlockSpec((tm, qw), lambda t: (t, 0)),
            pl.BlockSpec((tm, vw), lambda t: (t, 0)),
        ],
        out_shape=[
            jax.ShapeDtypeStruct((tk.n_tok, qw), BF16),
            jax.ShapeDtypeStruct((tk.n_tok, qw), BF16),
            jax.ShapeDtypeStruct((tk.n_tok, vw), BF16),
        ],
        compiler_params=_cparams("parallel"),
        name="mla_proj",
    )(X, g.reshape(1, D), mod, mod, wqd, qng.reshape(1, -1), wqu, wkvd, kvng.reshape(1, -1), wkvu, qg, kg,
      jnp.asarray(cos), jnp.asarray(sin))


def _attn_kernel(*refs, n_kv):
    q_ref = refs[0]
    k_refs = refs[1:1 + n_kv]
    v_refs = refs[1 + n_kv:1 + 2 * n_kv]
    o_ref = refs[-1]
    q = q_ref[...]
    scores = [_dot_nt(q, k[...]) for k in k_refs]
    m = functools.reduce(jnp.maximum, [jnp.max(s, axis=-1, keepdims=True) for s in scores])
    ps = [jnp.exp(s - m) for s in scores]
    denom = functools.reduce(jnp.add, [jnp.sum(p, axis=-1, keepdims=True) for p in ps])
    acc = functools.reduce(jnp.add, [_dot(p.astype(BF16), v[...]) for p, v in zip(ps, v_refs)])
    o_ref[...] = (acc * (1.0 / denom)).astype(BF16)


def _attention(tk, Q, K, V, latent, prev_out):
    B, L, LC = tk.B, tk.L, tk.LC
    H = MLA_HEADS
    ctx_blk0 = tk.n_lat // LC
    if latent:
        tq = tk.tm
        nq = L // tq
        grid = (B, H, nq)
        q_index = lambda b, h, i: (b * nq + i, h)
        k_specs = [pl.BlockSpec((L, HEAD_PAD), lambda b, h, i: (b, h)),
                   pl.BlockSpec((LC, HEAD_PAD), lambda b, h, i: (ctx_blk0 + b, h))]
        v_specs = [pl.BlockSpec((L, MLA_V), lambda b, h, i: (b, h)),
                   pl.BlockSpec((LC, MLA_V), lambda b, h, i: (ctx_blk0 + b, h))]
        sem = ("parallel", "parallel", "arbitrary")
    else:
        tq = LC
        grid = (B, H)
        q_index = lambda b, h: (ctx_blk0 + b, h)
        k_specs = [pl.BlockSpec((LC, HEAD_PAD), lambda b, h: (ctx_blk0 + b, h))]
        v_specs = [pl.BlockSpec((LC, MLA_V), lambda b, h: (ctx_blk0 + b, h))]
        sem = ("parallel", "parallel")
    n_kv = len(k_specs)
    in_specs = [pl.BlockSpec((tq, HEAD_PAD), q_index)] + k_specs + v_specs
    args = [Q] + [K] * n_kv + [V] * n_kv
    n_in = len(args)
    aliases = {}
    base = functools.partial(_attn_kernel, n_kv=n_kv)
    kern = base
    if prev_out is not None:
        in_specs.append(pl.BlockSpec(memory_space=pl.ANY))
        args.append(prev_out)
        aliases = {n_in: 0}
        kern = lambda *refs: base(*refs[:n_in], refs[n_in + 1])
    return pl.pallas_call(
        kern,
        grid=grid,
        in_specs=in_specs,
        out_specs=pl.BlockSpec((tq, MLA_V), q_index),
        out_shape=jax.ShapeDtypeStruct((tk.n_tok, H * MLA_V), BF16),
        input_output_aliases=aliases,
        compiler_params=_cparams(*sem),
        name="mla_attn_lat" if latent else "mla_attn_ctx",
    )(*args)


def kernel(x, c, ctx, c_ctx, ada_w, ada_b, norm_mix_g, norm_ffn_g, hy_in_w, hy_in_b, hy_conv_w, hy_conv_b,
           hy_f_w1, hy_f_b1, hy_f_w2, hy_f_b2, hy_f_w3, hy_sin_freq, hy_f_bias, hy_out_w, hy_out_b,
           mla_wq_down, mla_q_norm_g, mla_wq_up, mla_wkv_down, mla_kv_norm_g, mla_wkv_up, mla_q_head_g,
           mla_k_head_g, mla_wo, router_w, router_bias, exp_w1, exp_w3, exp_w2, sh_w1, sh_w3, sh_w2):
    B, L, D = x.shape
    LC = ctx.shape[1]
    tk = _Tokens(B, L, LC, D)
    assert B + 1 <= MOD_ROWS
    X = jnp.concatenate([x.reshape(B * L, D), ctx.reshape(B * LC, D)], axis=0)
    cc = jnp.zeros((MOD_ROWS, D), F32).at[:B].set(c).at[B].set(c_ctx)
    mods = _ada_mods(cc, ada_w, ada_b)

    rw_t = router_w.T.astype(BF16)
    rb = router_bias.reshape(N_EXPERTS, 1).astype(F32)
    dft = {n: tuple(jnp.asarray(t, dtype=BF16) for t in _dft_tables(n)) for n in sorted({L, LC})}

    for i in range(DEPTH):
        need_ctx = i < DEPTH - 1
        j = i // 2
        mod = mods[i]
        if i % 2 == 0:
            zpre = _hy_in(tk, X, mod, norm_mix_g[i], hy_in_w[j].astype(BF16), hy_in_b[j])
            Y = None
            for (n, blk0) in ((L, 0), (LC, tk.n_lat // LC)):
                cm, sm = dft[n]
                kr, ki, kn = _hy_filter(n, D, hy_f_w1[j], hy_f_b1[j], hy_f_w2[j], hy_f_b2[j], hy_f_w3[j],
                                        hy_sin_freq[j], cm, sm)
                Y = _hy_conv(tk, zpre, hy_conv_w[j], hy_conv_b[j].reshape(1, -1), hy_f_bias[j].reshape(1, D),
                             kr, ki, kn, cm, sm, n, blk0, Y)
            w_out, b_out = hy_out_w[j].astype(BF16), hy_out_b[j]
        else:
            wqu, wkvd, wkvu = _mla_weights(mla_wq_up[j], mla_wkv_down[j], mla_wkv_up[j])
            Q, K, V = _mla_proj(tk, X, mod, norm_mix_g[i], mla_wq_down[j].astype(BF16), mla_q_norm_g[j], wqu,
                                wkvd, mla_kv_norm_g[j], wkvu, _head_gain_rows(mla_q_head_g[j]),
                                _head_gain_rows(mla_k_head_g[j]))
            Y = _attention(tk, Q, K, V, True, None)
            if need_ctx:
                Y = _attention(tk, Q, K, V, False, Y)
            w_out, b_out = mla_wo[j].astype(BF16), jnp.zeros((D,), F32)

        n_tiles = tk.nt_all if need_ctx else tk.nt_lat
        Xn, h2, route = _post_mixer(tk, n_tiles, Y, w_out, b_out, X, mod, norm_ffn_g[i], rw_t, rb)
        w13 = jnp.concatenate([exp_w1[i], exp_w3[i]], axis=-1).astype(BF16)
        s13 = jnp.concatenate([sh_w1[i], sh_w3[i]], axis=-1).astype(BF16)
        f = _moe_layer(h2, route, w13, exp_w2[i].astype(BF16), s13, sh_w2[i].astype(BF16))
        g2 = mod[5, :, 0, :]
        gate = jnp.broadcast_to(g2[:B, None, :], (B, L, D)).reshape(tk.n_lat, D)
        if need_ctx:
            gate = jnp.concatenate([gate, jnp.broadcast_to(g2[B:B + 1], (tk.n_ctx, D))], axis=0)
        X = Xn + gate * f.astype(F32)
    return X[:tk.n_lat].reshape(B, L, D)
```

```python
import functools
import math

import numpy as np
import jax
import jax.numpy as jnp
from jax import lax
from jax.experimental import pallas as pl
from jax.experimental.pallas import tpu as pltpu

F32 = jnp.float32
BF16 = jnp.bfloat16

DEPTH = 4
GRID_W = 64
EPS = 1e-6

HY_EMB_BANDS = 16
HY_EMB_DIM = 1 + 2 * HY_EMB_BANDS
HY_HID = 64
HY_INNER_MLPS = 2
HY_DECAY_TARGET = 1e-2
HY_FAST_DECAY_PCT = 0.3
HY_SLOW_DECAY_PCT = 1.5

MLA_HEADS = 8
MLA_NOPE = 128
MLA_ROPE = 64
MLA_QK = MLA_NOPE + MLA_ROPE
MLA_V = 128
MLA_Q_LORA = 384
MLA_KV_LORA = 256
ROPE_BASE = 10000.0
SOFTMAX_SCALE = MLA_QK ** -0.5
HEAD_PAD = 256

N_EXPERTS = 16
N_GROUPS = 4
EXPERTS_PER_GROUP = 4
N_PAIRS = 6
N_CLASSES = N_GROUPS * N_PAIRS
PAIR_LO = (0, 0, 0, 1, 1, 2)
PAIR_HI = (1, 2, 3, 2, 3, 3)
D_EXPERT = 512

LANES = 128
ROW_SUB = 8
MOD_ROWS = 32
MOE_TILE = 256
CH_TILE = 256
VMEM_LIMIT = 56 * 1024 * 1024


def _cparams(*sem):
    return pltpu.CompilerParams(dimension_semantics=sem, vmem_limit_bytes=VMEM_LIMIT)


def _token_tile(L):
    return 512 if L % 512 == 0 else 256


def _sigmoid(v):
    return 1.0 / (1.0 + jnp.exp(-v))


def _norm_mod(x, g, shift, scale):
    ms = jnp.mean(x * x, axis=-1, keepdims=True)
    return (x * lax.rsqrt(ms + EPS) * g) * (1.0 + scale) + shift


def _split_bf16(v):
    hi = v.astype(BF16)
    lo = (v - hi.astype(F32)).astype(BF16)
    return hi, lo


def _dot(a, b):
    return jnp.dot(a, b, preferred_element_type=F32)


def _store_row_tiles(ref, v):
    for k in range(ROW_SUB):
        ref[:, k, :] = v[:, k * LANES:(k + 1) * LANES]


def _load_row_tiles(ref):
    return jnp.concatenate([ref[:, k, :] for k in range(ROW_SUB)], axis=1)


def _dot_nt(a, b):
    return lax.dot_general(a, b, (((1,), (1,)), ((), ())), preferred_element_type=F32)


def _ada_kernel(cc_ref, w_ref, b_ref, o_ref):
    cc = cc_ref[...]
    s = cc * _sigmoid(cc)
    s_hi, s_lo = _split_bf16(s)
    w_hi, w_lo = _split_bf16(w_ref[0])
    o_ref[0] = _dot(s_hi, w_hi) + _dot(s_hi, w_lo) + _dot(s_lo, w_hi) + b_ref[0]


def _ada_mods(cc, ada_w, ada_b):
    depth, d, d6 = ada_w.shape
    tn = 1536
    out = pl.pallas_call(
        _ada_kernel,
        grid=(depth, d6 // tn),
        in_specs=[
            pl.BlockSpec((MOD_ROWS, d), lambda i, j: (0, 0)),
            pl.BlockSpec((1, d, tn), lambda i, j: (i, 0, j)),
            pl.BlockSpec((1, 1, tn), lambda i, j: (i, 0, j)),
        ],
        out_specs=pl.BlockSpec((1, MOD_ROWS, tn), lambda i, j: (i, 0, j)),
        out_shape=jax.ShapeDtypeStruct((depth, MOD_ROWS, d6), F32),
        compiler_params=_cparams("arbitrary", "arbitrary"),
        name="ada_mods",
    )(cc, ada_w, ada_b.reshape(depth, 1, d6))
    return out.reshape(depth, MOD_ROWS, 6, 1, d).transpose(0, 2, 1, 3, 4)


class _Tokens:
    def __init__(self, B, L, LC, D):
        self.B, self.L, self.LC, self.D = B, L, LC, D
        self.n_lat = B * L
        self.n_ctx = B * LC
        self.n_tok = self.n_lat + self.n_ctx
        self.tm = _token_tile(L)
        assert L % self.tm == 0 and self.n_ctx % self.tm == 0 and L % LC == 0
        self.nt_lat = self.n_lat // self.tm
        self.nt_all = self.n_tok // self.tm

    def mod_spec(self, kind):
        tm, L, n_lat, B = self.tm, self.L, self.n_lat, self.B

        def index(t):
            row = jnp.where(t * tm < n_lat, (t * tm) // L, B)
            return (kind, row, 0, 0)

        return pl.BlockSpec((None, None, 1, self.D), index)


def _row_spec(width):
    return pl.BlockSpec((1, width), lambda t: (0, 0))


def _residual_in(tk, resid):
    D, tm = tk.D, tk.tm
    x_spec = pl.BlockSpec((tm, D), lambda t: (t, 0))
    if len(resid) == 1:
        return [x_spec], list(resid)
    xn, f3, mod_prev = resid
    return ([x_spec, pl.BlockSpec((tm, ROW_SUB, LANES), lambda t: (t, 0, 0)), tk.mod_spec(5)],
            [xn, f3, mod_prev])


def _residual_value(refs, n_in, xo_ref):
    if n_in == 1:
        return refs[0][...]
    x_ref, f_ref, g2_ref = refs
    x = x_ref[...] + g2_ref[...] * _load_row_tiles(f_ref)
    xo_ref[...] = x
    return x


def _hy_in_kernel(*refs, n_in):
    g_ref, sh_ref, sc_ref, w_ref, b_ref = refs[n_in:n_in + 5]
    outs = refs[n_in + 5:]
    x = _residual_value(refs[:n_in], n_in, outs[0])
    h = _norm_mod(x, g_ref[...], sh_ref[...], sc_ref[...]).astype(BF16)
    outs[-1][...] = (_dot(h, w_ref[...]) + b_ref[...]).astype(BF16)


def _hy_in(tk, resid, mod, g, w_bf16, b):
    D, tm = tk.D, tk.tm
    n_out = w_bf16.shape[1]
    r_specs, r_args = _residual_in(tk, resid)
    fused = len(resid) > 1
    x_out_spec = [pl.BlockSpec((tm, D), lambda t: (t, 0))] if fused else []
    x_out_shape = [jax.ShapeDtypeStruct((tk.n_tok, D), F32)] if fused else []
    outs = pl.pallas_call(
        functools.partial(_hy_in_kernel, n_in=len(r_args)),
        grid=(tk.nt_all,),
        in_specs=r_specs + [
            _row_spec(D),
            tk.mod_spec(0),
            tk.mod_spec(1),
            pl.BlockSpec((D, n_out), lambda t: (0, 0)),
            _row_spec(n_out),
        ],
        out_specs=x_out_spec + [pl.BlockSpec((tm, n_out), lambda t: (t, 0))],
        out_shape=x_out_shape + [jax.ShapeDtypeStruct((tk.n_tok, n_out), BF16)],
        compiler_params=_cparams("parallel"),
        name="hy_in_proj",
    )(*r_args, g.reshape(1, D), mod, mod, w_bf16, b.reshape(1, n_out))
    return (outs[0], outs[1]) if fused else (resid[0], outs[0])


@functools.lru_cache(maxsize=None)
def _dft_tables(L):
    idx = np.arange(L, dtype=np.int64)
    prod = np.mod(np.outer(idx, idx), 2 * L).astype(np.float64)
    ang = prod * (math.pi / L)
    return np.cos(ang).astype(np.float32), np.sin(ang).astype(np.float32)


@functools.lru_cache(maxsize=None)
def _filter_tables(L, D):
    t_norm = np.linspace(0.0, 1.0, L, dtype=np.float64)[:, None]
    pos = np.arange(L, dtype=np.float64)[:, None]
    bands = np.linspace(1e-4, HY_EMB_BANDS - 1, HY_EMB_BANDS, dtype=np.float64)[None, :]
    ang = 2.0 * math.pi * bands * pos / L
    z = np.concatenate([t_norm, np.cos(ang), -np.sin(ang)], axis=-1)
    z = np.pad(z, ((0, 0), (0, HY_HID - HY_EMB_DIM)))
    max_decay = math.log(HY_DECAY_TARGET) / HY_FAST_DECAY_PCT
    min_decay = math.log(HY_DECAY_TARGET) / HY_SLOW_DECAY_PCT
    deltas = np.linspace(min_decay, max_decay, D, dtype=np.float64)
    decay = np.exp(-t_norm * np.abs(deltas))
    rev = (L - np.arange(L)) % L
    z_rev = z[rev]
    decay_rev = decay[rev].copy()
    decay_rev[0] = 0.0
    f32 = lambda a: np.asarray(a, dtype=np.float32)
    return f32(z), f32(z_rev), f32(decay), f32(decay_rev)


def _hy_filter_kernel(z_ref, zr_ref, dec_ref, decr_ref, w1_ref, b1_ref, w2_ref, b2_ref, w3a_ref, w3b_ref,
                      fr_ref, c_ref, s_ref, kr_ref, ki_ref, kn_ref):
    hp = lax.Precision.HIGHEST
    fr = fr_ref[...]

    def dot_hp(a, b):
        return jnp.dot(a, b, precision=hp, preferred_element_type=F32)

    def mlp(z):
        h = jnp.sin(fr * (dot_hp(z, w1_ref[...]) + b1_ref[...]))
        for m in range(HY_INNER_MLPS):
            h = jnp.sin(fr * (dot_hp(h, w2_ref[m]) + b2_ref[m]))
        return h

    ka = dot_hp(mlp(z_ref[...]), w3a_ref[...]) * dec_ref[...]
    kb = dot_hp(mlp(zr_ref[...]), w3b_ref[...]) * decr_ref[...]
    norm = jnp.sum(jnp.abs(ka), axis=0, keepdims=True) + jnp.sum(jnp.abs(kb), axis=0, keepdims=True)
    inv = 1.0 / norm
    ka = ka * inv
    kb = kb * inv
    L = ka.shape[0]
    inv_n = 1.0 / (2 * L)
    row = lax.broadcasted_iota(jnp.int32, ka.shape, 0)
    alt = jnp.where((row & 1) == 0, 1.0, -1.0)
    kn_ref[...] = jnp.sum(alt * (ka + kb), axis=0, keepdims=True) * inv_n
    ka_b = ka.astype(BF16)
    kb_b = kb.astype(BF16)
    cm = c_ref[...]
    sm = s_ref[...]
    k_cos = _dot(cm, ka_b) + alt * _dot(cm, kb_b)
    k_sin = _dot(sm, ka_b) + alt * _dot(sm, kb_b)
    a = jnp.where(row == 0, inv_n, 2.0 * inv_n)
    kr_ref[...] = a * k_cos
    ki_ref[...] = -(a * k_sin)


def _hy_filter(L, D, fw1, fb1, fw2, fb2, fw3, freq, cm, sm):
    z, z_rev, decay, decay_rev = _filter_tables(L, D)
    ct = CH_TILE
    nj = D // ct
    w1p = jnp.pad(fw1, ((0, HY_HID - HY_EMB_DIM), (0, 0)))
    full = lambda shape: pl.BlockSpec(shape, lambda j: (0,) * len(shape))
    single = pl.Buffered(1)
    return pl.pallas_call(
        _hy_filter_kernel,
        grid=(nj,),
        in_specs=[
            full((L, HY_HID)),
            full((L, HY_HID)),
            pl.BlockSpec((L, ct), lambda j: (0, j)),
            pl.BlockSpec((L, ct), lambda j: (0, j)),
            full((HY_HID, HY_HID)),
            full((1, HY_HID)),
            full((HY_INNER_MLPS, HY_HID, HY_HID)),
            full((HY_INNER_MLPS, 1, HY_HID)),
            pl.BlockSpec((HY_HID, ct), lambda j: (0, j)),
            pl.BlockSpec((HY_HID, ct), lambda j: (0, nj + j)),
            full((1, HY_HID)),
            pl.BlockSpec((L, L), lambda j: (0, 0), pipeline_mode=single),
            pl.BlockSpec((L, L), lambda j: (0, 0), pipeline_mode=single),
        ],
        out_specs=[
            pl.BlockSpec((L, ct), lambda j: (0, j)),
            pl.BlockSpec((L, ct), lambda j: (0, j)),
            pl.BlockSpec((1, ct), lambda j: (0, j)),
        ],
        out_shape=[
            jax.ShapeDtypeStruct((L, D), F32),
            jax.ShapeDtypeStruct((L, D), F32),
            jax.ShapeDtypeStruct((1, D), F32),
        ],
        compiler_params=_cparams("arbitrary"),
        name=f"hy_filter_{L}",
    )(jnp.asarray(z), jnp.asarray(z_rev), jnp.asarray(decay), jnp.asarray(decay_rev), w1p,
      fb1.reshape(1, HY_HID), fw2, fb2.reshape(HY_INNER_MLPS, 1, HY_HID), fw3, fw3,
      freq.reshape(1, HY_HID), cm, sm)


def _hy_conv_kernel(z0_ref, z1_ref, zv_ref, cw0_ref, cw1_ref, cwv_ref, cb0_ref, cb1_ref, cbv_ref, fb_ref,
                    kr_ref, ki_ref, kn_ref, c_ref, s_ref, o_ref):
    L = z0_ref.shape[0]
    row = lax.broadcasted_iota(jnp.int32, z0_ref.shape, 0)
    first = row == 0
    last = row == L - 1

    def conv3(z_ref, w_ref, b_ref):
        z = z_ref[...].astype(F32)
        w = w_ref[...]
        prev = jnp.where(first, 0.0, pltpu.roll(z, 1, 0))
        nxt = jnp.where(last, 0.0, pltpu.roll(z, L - 1, 0))
        return prev * w[0:1] + z * w[1:2] + nxt * w[2:3] + b_ref[...]

    vg = conv3(zv_ref, cwv_ref, cbv_ref) * conv3(z1_ref, cw1_ref, cb1_ref)
    alt = jnp.where((row & 1) == 0, 1.0, -1.0)
    x_nyq = jnp.sum(alt * vg, axis=0, keepdims=True)
    vb = vg.astype(BF16)
    cm = c_ref[...]
    sm = s_ref[...]
    xc = _dot(cm, vb)
    xs = _dot(sm, vb)
    kr = kr_ref[...]
    ki = ki_ref[...]
    u = (xc * kr + xs * ki).astype(BF16)
    w = (xs * kr - xc * ki).astype(BF16)
    y = _dot(cm, u) + _dot(sm, w) + alt * (x_nyq * kn_ref[...])
    y = y + vg * fb_ref[...]
    o_ref[...] = (conv3(z0_ref, cw0_ref, cb0_ref) * y).astype(BF16)


def _hy_conv(tk, zpre, conv_w, conv_b, f_bias, kr, ki, kn, cm, sm, L, row_block0):
    D, B = tk.D, tk.B
    ct = CH_TILE
    nj = D // ct
    single = pl.Buffered(1)
    zspec = lambda part: pl.BlockSpec((L, ct), lambda j, b: (row_block0 + b, part * nj + j))
    wspec = lambda part: pl.BlockSpec((3, ct), lambda j, b: (0, part * nj + j))
    bspec = lambda part: pl.BlockSpec((1, ct), lambda j, b: (0, part * nj + j))
    in_specs = [
        zspec(0), zspec(1), zspec(2),
        wspec(0), wspec(1), wspec(2),
        bspec(0), bspec(1), bspec(2),
        pl.BlockSpec((1, ct), lambda j, b: (0, j)),
        pl.BlockSpec((L, ct), lambda j, b: (0, j), pipeline_mode=single),
        pl.BlockSpec((L, ct), lambda j, b: (0, j), pipeline_mode=single),
        pl.BlockSpec((1, ct), lambda j, b: (0, j)),
        pl.BlockSpec((L, L), lambda j, b: (0, 0), pipeline_mode=single),
        pl.BlockSpec((L, L), lambda j, b: (0, 0), pipeline_mode=single),
    ]
    args = [zpre, zpre, zpre, conv_w, conv_w, conv_w, conv_b, conv_b, conv_b, f_bias, kr, ki, kn, cm, sm]
    return pl.pallas_call(
        _hy_conv_kernel,
        grid=(nj, B),
        in_specs=in_specs,
        out_specs=pl.BlockSpec((L, ct), lambda j, b: (b, j)),
        out_shape=jax.ShapeDtypeStruct((B * L, D), BF16),
        compiler_params=_cparams("arbitrary", "arbitrary"),
        name=f"hy_conv_{L}",
    )(*args)


def _route_rows(s, sb):
    srow = [s[e:e + 1, :] for e in range(N_EXPERTS)]
    brow = [sb[e:e + 1, :] for e in range(N_EXPERTS)]
    gscore = []
    for g in range(N_GROUPS):
        a, b, c, d = brow[4 * g:4 * g + 4]
        m = jnp.maximum(jnp.maximum(a + b, a + c), jnp.maximum(a + d, b + c))
        gscore.append(jnp.maximum(m, jnp.maximum(b + d, c + d)))
    best = gscore[0]
    gi = jnp.zeros_like(best)
    for g in range(1, N_GROUPS):
        upd = gscore[g] > best
        best = jnp.where(upd, gscore[g], best)
        gi = jnp.where(upd, float(g), gi)

    def pick(rows, j):
        return jnp.where(gi == 0.0, rows[j],
                         jnp.where(gi == 1.0, rows[4 + j], jnp.where(gi == 2.0, rows[8 + j], rows[12 + j])))

    v = [pick(brow, j) for j in range(4)]
    u = [pick(srow, j) for j in range(4)]
    sel = []
    for j in range(4):
        cnt = jnp.zeros_like(best)
        for k in range(4):
            if k == j:
                continue
            beats = (v[k] >= v[j]) if k < j else (v[k] > v[j])
            cnt = cnt + jnp.where(beats, 1.0, 0.0)
        sel.append(cnt < 2.0)
    lo = jnp.where(sel[0], 0.0, jnp.where(sel[1], 1.0, 2.0))
    hi = jnp.where(sel[3], 3.0, jnp.where(sel[2], 2.0, 1.0))
    w_lo = jnp.where(sel[0], u[0], jnp.where(sel[1], u[1], u[2]))
    w_hi = jnp.where(sel[3], u[3], jnp.where(sel[2], u[2], u[1]))
    pair = lo * (7.0 - lo) * 0.5 + (hi - lo - 1.0)
    tot = w_lo + w_hi
    return gi * float(N_PAIRS) + pair, w_lo / tot, w_hi / tot


def _post_kernel(*refs, nt_lat):
    (w_ref, b_ref, x_ref, g1_ref, ng_ref, sh_ref, sc_ref, rw_ref, rb_ref, xo_ref, h_ref, r_ref) = refs[-12:]
    if len(refs) == 13:
        y = refs[0][...]
    else:
        y = jnp.where(pl.program_id(0) < nt_lat, refs[0][...], refs[1][...])
    mix = _dot(y, w_ref[...]) + b_ref[...]
    xn = x_ref[...] + g1_ref[...] * mix
    xo_ref[...] = xn
    h2 = _norm_mod(xn, ng_ref[...], sh_ref[...], sc_ref[...]).astype(BF16)
    _store_row_tiles(h_ref, h2.astype(F32))
    s = _sigmoid(_dot_nt(rw_ref[...], h2))
    cls, w_lo, w_hi = _route_rows(s, s + rb_ref[...])
    r_ref[0, 0:1, :] = cls
    r_ref[0, 1:2, :] = w_lo
    r_ref[0, 2:3, :] = w_hi
    r_ref[0, 3:8, :] = jnp.zeros((5, cls.shape[1]), F32)


def _post_mixer(tk, ys, w_bf16, b, X, mod, ng, rw_t, rb):
    D, tm = tk.D, tk.tm
    K = w_bf16.shape[0]
    nt_lat = tk.nt_lat
    if len(ys) == 1:
        n_tiles = nt_lat
        y_specs = [pl.BlockSpec((tm, K), lambda t: (t, 0))]
    else:
        n_tiles = tk.nt_all
        y_specs = [pl.BlockSpec((tm, K), lambda t: (jnp.minimum(t, nt_lat - 1), 0)),
                   pl.BlockSpec((tm, K), lambda t: (jnp.maximum(t - nt_lat, 0), 0))]
    n_rows = n_tiles * tm
    return pl.pallas_call(
        functools.partial(_post_kernel, nt_lat=nt_lat),
        grid=(n_tiles,),
        in_specs=y_specs + [
            pl.BlockSpec((K, D), lambda t: (0, 0)),
            _row_spec(D),
            pl.BlockSpec((tm, D), lambda t: (t, 0)),
            tk.mod_spec(2),
            _row_spec(D),
            tk.mod_spec(3),
            tk.mod_spec(4),
            pl.BlockSpec((N_EXPERTS, D), lambda t: (0, 0)),
            pl.BlockSpec((N_EXPERTS, 1), lambda t: (0, 0)),
        ],
        out_specs=[
            pl.BlockSpec((tm, D), lambda t: (t, 0)),
            pl.BlockSpec((tm, ROW_SUB, LANES), lambda t: (t, 0, 0)),
            pl.BlockSpec((1, 8, tm), lambda t: (t, 0, 0)),
        ],
        out_shape=[
            jax.ShapeDtypeStruct((n_rows, D), F32),
            jax.ShapeDtypeStruct((n_rows, ROW_SUB, LANES), F32),
            jax.ShapeDtypeStruct((n_tiles, 8, tm), F32),
        ],
        compiler_params=_cparams("parallel"),
        name="post_mixer",
    )(*ys, w_bf16, b.reshape(1, D), X, mod, ng.reshape(1, D), mod, mod, rw_t, rb)


def _moe_kernel(ea_ref, eb_ref, nu_ref, gi_ref, si_ref, h3_ref, rwa_ref, rwb_ref, a13_ref, a2_ref, b13_ref, b2_ref,
                s13_ref, s2_ref, f3_ref, hbuf, obuf, gsem, ssem, *, n_tok):
    t = pl.program_id(0)
    n_used = nu_ref[0]
    last_tile = pl.num_programs(0) - 1
    T = MOE_TILE
    slot = lax.rem(t, 2)
    other = 1 - slot

    def gather_start(tile, sl):
        base = tile * T
        for r in range(T):
            pltpu.make_async_copy(h3_ref.at[gi_ref[base + r]], hbuf.at[sl, r], gsem.at[sl]).start()

    def gather_wait(sl):
        pltpu.make_async_copy(h3_ref.at[pl.ds(0, T)], hbuf.at[sl], gsem.at[sl]).wait()

    def scatter_start(tile, sl):
        base = tile * T
        for r in range(T):
            pltpu.make_async_copy(obuf.at[sl, r], f3_ref.at[si_ref[base + r]], ssem.at[0]).start()

    def scatter_wait(sl):
        pltpu.make_async_copy(obuf.at[sl], f3_ref.at[pl.ds(0, T)], ssem.at[0]).wait()

    @pl.when(t == 0)
    def _():
        gather_start(0, 0)
        obuf[1] = jnp.zeros(obuf.shape[1:], F32)
        pltpu.make_async_copy(obuf.at[1], f3_ref.at[pl.ds(n_tok, T)], ssem.at[0]).start()

    @pl.when(t < n_used)
    def _():
        gather_start(jnp.minimum(t + 1, last_tile), other)
        gather_wait(slot)
        h = _load_row_tiles(hbuf.at[slot]).astype(BF16)

        sa = _sigmoid(_dot_nt(rwa_ref[...], h)[0:1])
        sb = _sigmoid(_dot_nt(rwb_ref[...], h)[0:1])
        tot = sa + sb
        row = lax.broadcasted_iota(jnp.int32, (ROW_SUB, T), 0)
        gate_rows = jnp.where(row == 0, sa / tot, jnp.where(row == 1, sb / tot, 0.0))
        gate_cols = gate_rows.T

        def ffn(w13_ref, w2_ref, gate):
            a = _dot(h, w13_ref[...])
            de = a.shape[1] // 2
            a1 = a[:, :de]
            hid = a1 * _sigmoid(a1) * a[:, de:]
            if gate is not None:
                hid = hid * gate
            return _dot(hid.astype(BF16), w2_ref[...])

        out = (ffn(s13_ref, s2_ref, None) + ffn(a13_ref, a2_ref, gate_cols[:, 0:1])
               + ffn(b13_ref, b2_ref, gate_cols[:, 1:2]))
        _store_row_tiles(obuf.at[slot], out)
        scatter_wait(other)
        scatter_start(t, slot)

    @pl.when(t == n_used - 1)
    def _():
        scatter_wait(slot)
        gather_wait(other)


def _moe(h3, gi, si, tile_ea, tile_eb, n_used, rw16, w13, w2, s13, s2):
    n_tok = h3.shape[0]
    D = ROW_SUB * LANES
    tmm = MOE_TILE
    n_tiles = gi.shape[0] // tmm
    de2 = w13.shape[2]
    de = w2.shape[1]
    ix = lambda f: (lambda t, ea, eb, nu, gi_, si_: f(t, ea, eb))
    grid_spec = pltpu.PrefetchScalarGridSpec(
        num_scalar_prefetch=5,
        grid=(n_tiles,),
        in_specs=[
            pl.BlockSpec(memory_space=pl.ANY),
            pl.BlockSpec((None, 16, D), ix(lambda t, ea, eb: (ea[t], 0, 0))),
            pl.BlockSpec((None, 16, D), ix(lambda t, ea, eb: (eb[t], 0, 0))),
            pl.BlockSpec((None, D, de2), ix(lambda t, ea, eb: (ea[t], 0, 0))),
            pl.BlockSpec((None, de, D), ix(lambda t, ea, eb: (ea[t], 0, 0))),
            pl.BlockSpec((None, D, de2), ix(lambda t, ea, eb: (eb[t], 0, 0))),
            pl.BlockSpec((None, de, D), ix(lambda t, ea, eb: (eb[t], 0, 0))),
            pl.BlockSpec(s13.shape, ix(lambda t, ea, eb: (0, 0))),
            pl.BlockSpec(s2.shape, ix(lambda t, ea, eb: (0, 0))),
        ],
        out_specs=pl.BlockSpec(memory_space=pl.ANY),
        scratch_shapes=[
            pltpu.VMEM((2, tmm, ROW_SUB, LANES), F32),
            pltpu.VMEM((2, tmm, ROW_SUB, LANES), F32),
            pltpu.SemaphoreType.DMA((2,)),
            pltpu.SemaphoreType.DMA((1,)),
        ],
    )
    return pl.pallas_call(
        functools.partial(_moe_kernel, n_tok=n_tok),
        grid_spec=grid_spec,
        out_shape=jax.ShapeDtypeStruct((n_tok + tmm, ROW_SUB, LANES), F32),
        compiler_params=_cparams("arbitrary"),
        name="moe_pairs",
    )(tile_ea, tile_eb, n_used, gi, si, h3, rw16, rw16, w13, w2, w13, w2, s13, s2)


def _moe_layer(h3, route, rw16, w13, w2, s13, s2):
    n = h3.shape[0]
    tmm = MOE_TILE
    cls = route[:, 0, :].reshape(n).astype(jnp.int32)
    onehot =(cls[:, None] == jnp.arange(N_CLASSES, dtype=jnp.int32)[None, :]).astype(jnp.int32)
    csum = jnp.cumsum(onehot, axis=0)
    rank = jnp.sum(onehot * csum, axis=1) - 1
    counts = csum[-1]
    padded = ((counts + tmm - 1) // tmm) * tmm
    ends = jnp.cumsum(padded)
    starts = ends - padded
    pos = starts[cls] + rank
    n_tiles = (n + N_CLASSES * (tmm - 1)) // tmm + 1
    ns = n_tiles * tmm
    inv = jnp.zeros((ns,), jnp.int32).at[pos].set(jnp.arange(n, dtype=jnp.int32))
    last_cls = jnp.max(jnp.where(counts > 0, jnp.arange(N_CLASSES, dtype=jnp.int32), 0))
    tile_start = jnp.arange(n_tiles, dtype=jnp.int32) * tmm
    tile_cls = jnp.sum((tile_start[:, None] >= ends[None, :]).astype(jnp.int32), axis=1)
    tile_cls = jnp.minimum(tile_cls, last_cls)
    grp = tile_cls // N_PAIRS
    pair = tile_cls % N_PAIRS
    tile_ea = grp * EXPERTS_PER_GROUP + jnp.asarray(PAIR_LO, jnp.int32)[pair]
    tile_eb = grp * EXPERTS_PER_GROUP + jnp.asarray(PAIR_HI, jnp.int32)[pair]
    n_used = (ends[-1] // tmm).astype(jnp.int32).reshape(1)
    slot = jnp.arange(ns, dtype=jnp.int32)
    slot_cls = jnp.repeat(tile_cls, tmm)
    valid = slot < (starts + counts)[slot_cls]
    scatter_idx = jnp.where(valid, inv, n + slot % tmm)
    return _moe(h3, inv, scatter_idx, tile_ea, tile_eb, n_used, rw16, w13, w2, s13, s2)


def _rope_swap_perm():
    q = MLA_ROPE // 4
    p = np.arange(MLA_ROPE)
    return np.where((p % (2 * q)) < q, p + q, p - q)


@functools.lru_cache(maxsize=None)
def _rope_tables(L, tm):
    rows = L // GRID_W
    t_row = np.repeat(np.arange(rows, dtype=np.float64), GRID_W)
    t_col = np.tile(np.arange(GRID_W, dtype=np.float64), rows)
    half = MLA_ROPE // 2
    q = half // 2
    inv = (ROPE_BASE ** (-np.arange(0, half, 2, dtype=np.float32) / np.float32(half))).astype(np.float64)
    ang = np.concatenate([t_row[:, None] * inv, t_row[:, None] * inv,
                          t_col[:, None] * inv, t_col[:, None] * inv], axis=1)
    sign = np.tile(np.concatenate([-np.ones(q), np.ones(q)]), 2)[None, :]
    cos = np.zeros((L + tm, LANES), np.float32)
    sin = np.zeros((L + tm, LANES), np.float32)
    cos[:L, :MLA_ROPE] = np.cos(ang)
    sin[:L, :MLA_ROPE] = np.sin(ang) * sign
    cos[L:, :MLA_ROPE] = 1.0
    return cos, sin


def _mla_proj_kernel(*refs, n_in):
    (g_ref, sh_ref, sc_ref, wqd_ref, qng_ref, wqu_ref, wkvd_ref, kvng_ref, wkvu_ref,
     qg_ref, kg_ref, cos_ref, sin_ref) = refs[n_in:n_in + 13]
    outs = refs[n_in + 13:]
    q_ref, k_ref, v_ref = outs[-3:]
    x = _residual_value(refs[:n_in], n_in, outs[0])
    h = _norm_mod(x, g_ref[...], sh_ref[...], sc_ref[...]).astype(BF16)
    cos = cos_ref[...]
    sin = sin_ref[...]
    nq = MLA_HEADS * HEAD_PAD

    def rms(v, g):
        return v * lax.rsqrt(jnp.mean(v * v, axis=-1, keepdims=True) + EPS) * g

    q_lat = rms(_dot(h, wqd_ref[...]), qng_ref[...]).astype(BF16)
    q_all = _dot(q_lat, wqu_ref[...])
    qg = qg_ref[...]
    g_nope, gc, gs = qg[0:1], cos * qg[1:2], sin * qg[2:3]
    for hd in range(MLA_HEADS):
        nope = q_all[:, hd * HEAD_PAD:hd * HEAD_PAD + LANES]
        rope = q_all[:, hd * HEAD_PAD + LANES:(hd + 1) * HEAD_PAD]
        part = q_all[:, nq + hd * LANES:nq + (hd + 1) * LANES]
        ssq = jnp.sum(nope * nope, axis=-1, keepdims=True) + jnp.sum(rope * rope, axis=-1, keepdims=True)
        r = lax.rsqrt(ssq * (1.0 / MLA_QK) + EPS) * SOFTMAX_SCALE
        q_ref[:, hd * HEAD_PAD:hd * HEAD_PAD + LANES] = (nope * r * g_nope).astype(BF16)
        q_ref[:, hd * HEAD_PAD + LANES:(hd + 1) * HEAD_PAD] = (r * (rope * gc + part * gs)).astype(BF16)

    kv = _dot(h, wkvd_ref[...])
    c_kv = rms(kv[:, :MLA_KV_LORA], kvng_ref[...]).astype(BF16)
    k_pe = kv[:, MLA_KV_LORA:MLA_KV_LORA + LANES]
    k_pe_part = kv[:, MLA_KV_LORA + LANES:MLA_KV_LORA + 2 * LANES]
    kvu = _dot(c_kv, wkvu_ref[...])
    kg = kg_ref[...]
    gk_nope, gkc, gks = kg[0:1], cos * kg[1:2], sin * kg[2:3]
    pe_ssq = jnp.sum(k_pe * k_pe, axis=-1, keepdims=True)
    pe_rot = k_pe * gkc + k_pe_part * gks
    for hd in range(MLA_HEADS):
        nope = kvu[:, hd * LANES:(hd + 1) * LANES]
        ssq = jnp.sum(nope * nope, axis=-1, keepdims=True) + pe_ssq
        r = lax.rsqrt(ssq * (1.0 / MLA_QK) + EPS)
        k_ref[:, hd * HEAD_PAD:hd * HEAD_PAD + LANES] = (nope * r * gk_nope).astype(BF16)
        k_ref[:, hd * HEAD_PAD + LANES:(hd + 1) * HEAD_PAD] = (r * pe_rot).astype(BF16)
    v_ref[...] = kvu[:, MLA_HEADS * LANES:].astype(BF16)


def _head_gain_rows(g):
    perm = _rope_swap_perm()
    pad = lambda v: jnp.pad(v, (0, LANES - MLA_ROPE))
    rows = jnp.stack([g[:MLA_NOPE], pad(g[MLA_NOPE:]), pad(g[MLA_NOPE:][perm])])
    return jnp.pad(rows, ((0, 8 - 3), (0, 0)))


def _mla_weights(wq_up, wkv_down, wkv_up):
    perm = _rope_swap_perm()
    H = MLA_HEADS
    r = wq_up.shape[0]
    wq = wq_up.reshape(r, H, MLA_QK)
    nope, rope = wq[:, :, :MLA_NOPE], wq[:, :, MLA_NOPE:]
    zeros = jnp.zeros((r, H, LANES - MLA_ROPE), wq_up.dtype)
    main = jnp.concatenate([nope, rope, zeros], axis=-1).reshape(r, H * HEAD_PAD)
    part = jnp.concatenate([rope[:, :, perm], zeros], axis=-1).reshape(r, H * LANES)
    wqu = jnp.concatenate([main, part], axis=1).astype(BF16)
    d = wkv_down.shape[0]
    c, pe = wkv_down[:, :MLA_KV_LORA], wkv_down[:, MLA_KV_LORA:]
    z = jnp.zeros((d, LANES - MLA_ROPE), wkv_down.dtype)
    wkvd = jnp.concatenate([c, pe, z, pe[:, perm], z], axis=1).astype(BF16)
    wu = wkv_up.reshape(wkv_up.shape[0], H, MLA_NOPE + MLA_V)
    wkvu = jnp.concatenate([wu[:, :, :MLA_NOPE].reshape(-1, H * MLA_NOPE),
                            wu[:, :, MLA_NOPE:].reshape(-1, H * MLA_V)], axis=1).astype(BF16)
    return wqu, wkvd, wkvu


def _mla_proj(tk, resid, mod, g, wqd, qng, wqu, wkvd, kvng, wkvu, qg, kg):
    D, tm, L = tk.D, tk.tm, tk.L
    r_specs, r_args = _residual_in(tk, resid)
    fused = len(resid) > 1
    x_out_spec = [pl.BlockSpec((tm, D), lambda t: (t, 0))] if fused else []
    x_out_shape = [jax.ShapeDtypeStruct((tk.n_tok, D), F32)] if fused else []
    cos, sin = _rope_tables(L, tm)
    n_lat, per_seq = tk.n_lat, L // tm

    def table_index(t):
        return (jnp.where(t * tm < n_lat, t % per_seq, per_seq), 0)

    const = lambda a: pl.BlockSpec(a.shape, lambda t: (0,) * a.ndim)
    qw = MLA_HEADS * HEAD_PAD
    vw = MLA_HEADS * MLA_V
    outs = pl.pallas_call(
        functools.partial(_mla_proj_kernel, n_in=len(r_args)),
        grid=(tk.nt_all,),
        in_specs=r_specs + [
            _row_spec(D),
            tk.mod_spec(0),
            tk.mod_spec(1),
            const(wqd), _row_spec(MLA_Q_LORA), const(wqu), const(wkvd), _row_spec(MLA_KV_LORA), const(wkvu),
            const(qg), const(kg),
            pl.BlockSpec((tm, LANES), table_index),
            pl.BlockSpec((tm, LANES), table_index),
        ],
        out_specs=x_out_spec + [
            pl.BlockSpec((tm, qw), lambda t: (t, 0)),
            pl.BlockSpec((tm, qw), lambda t: (t, 0)),
            pl.BlockSpec((tm, vw), lambda t: (t, 0)),
        ],
        out_shape=x_out_shape + [
            jax.ShapeDtypeStruct((tk.n_tok, qw), BF16),
            jax.ShapeDtypeStruct((tk.n_tok, qw), BF16),
            jax.ShapeDtypeStruct((tk.n_tok, vw), BF16),
        ],
        compiler_params=_cparams("parallel"),
        name="mla_proj",
    )(*r_args, g.reshape(1, D), mod, mod, wqd, qng.reshape(1, -1), wqu, wkvd, kvng.reshape(1, -1), wkvu, qg, kg,
      jnp.asarray(cos), jnp.asarray(sin))
    return tuple(outs) if fused else (resid[0],) + tuple(outs)


def _attn_kernel(*refs, n_kv):
    q_ref = refs[0]
    k_refs = refs[1:1 + n_kv]
    v_refs = refs[1 + n_kv:1 + 2 * n_kv]
    o_ref = refs[-1]
    q = q_ref[...]
    scores = [_dot_nt(q, k[...]) for k in k_refs]
    m = functools.reduce(jnp.maximum, [jnp.max(s, axis=-1, keepdims=True) for s in scores])
    ps = [jnp.exp(s - m) for s in scores]
    denom = functools.reduce(jnp.add, [jnp.sum(p, axis=-1, keepdims=True) for p in ps])
    acc = functools.reduce(jnp.add, [_dot(p.astype(BF16), v[...]) for p, v in zip(ps, v_refs)])
    o_ref[...] = (acc * (1.0 / denom)).astype(BF16)


def _attention(tk, Q, K, V, latent):
    B, L, LC = tk.B, tk.L, tk.LC
    H = MLA_HEADS
    ctx_blk0 = tk.n_lat // LC
    if latent:
        tq = tk.tm
        nq = L // tq
        grid = (B, H, nq)
        q_index = lambda b, h, i: (b * nq + i, h)
        k_specs = [pl.BlockSpec((L, HEAD_PAD), lambda b, h, i: (b, h)),
                   pl.BlockSpec((LC, HEAD_PAD), lambda b, h, i: (ctx_blk0 + b, h))]
        v_specs = [pl.BlockSpec((L, MLA_V), lambda b, h, i: (b, h)),
                   pl.BlockSpec((LC, MLA_V), lambda b, h, i: (ctx_blk0 + b, h))]
        o_index = q_index
        n_out = tk.n_lat
        sem = ("parallel", "parallel", "arbitrary")
    else:
        tq = LC
        grid = (B, H)
        q_index = lambda b, h: (ctx_blk0 + b, h)
        k_specs = [pl.BlockSpec((LC, HEAD_PAD), lambda b, h: (ctx_blk0 + b, h))]
        v_specs = [pl.BlockSpec((LC, MLA_V), lambda b, h: (ctx_blk0 + b, h))]
        o_index = lambda b, h: (b, h)
        n_out = tk.n_ctx
        sem = ("parallel", "parallel")
    n_kv = len(k_specs)
    return pl.pallas_call(
        functools.partial(_attn_kernel, n_kv=n_kv),
        grid=grid,
        in_specs=[pl.BlockSpec((tq, HEAD_PAD), q_index)] + k_specs + v_specs,
        out_specs=pl.BlockSpec((tq, MLA_V), o_index),
        out_shape=jax.ShapeDtypeStruct((n_out, H * MLA_V), BF16),
        compiler_params=_cparams(*sem),
        name="mla_attn_lat" if latent else "mla_attn_ctx",
    )(*([Q] + [K] * n_kv + [V] * n_kv))


def _final_kernel(x_ref, f_ref, g2_ref, o_ref):
    o_ref[...] = x_ref[...] + g2_ref[...] * _load_row_tiles(f_ref)


def _final_combine(tk, xn, f3, mod):
    D, tm = tk.D, tk.tm
    r_specs, r_args = _residual_in(tk, (xn, f3, mod))
    return pl.pallas_call(
        _final_kernel,
        grid=(tk.nt_lat,),
        in_specs=r_specs,
        out_specs=pl.BlockSpec((tm, D), lambda t: (t, 0)),
        out_shape=jax.ShapeDtypeStruct((tk.n_lat, D), F32),
        compiler_params=_cparams("parallel"),
        name="final_combine",
    )(*r_args)


def kernel(x, c, ctx, c_ctx, ada_w, ada_b, norm_mix_g, norm_ffn_g, hy_in_w, hy_in_b, hy_conv_w, hy_conv_b,
           hy_f_w1, hy_f_b1, hy_f_w2, hy_f_b2, hy_f_w3, hy_sin_freq, hy_f_bias, hy_out_w, hy_out_b,
           mla_wq_down, mla_q_norm_g, mla_wq_up, mla_wkv_down, mla_kv_norm_g, mla_wkv_up, mla_q_head_g,
           mla_k_head_g, mla_wo, router_w, router_bias, exp_w1, exp_w3, exp_w2, sh_w1, sh_w3, sh_w2):
    B, L, D = x.shape
    LC = ctx.shape[1]
    tk = _Tokens(B, L, LC, D)
    assert B + 1 <= MOD_ROWS
    X = jnp.concatenate([x.reshape(B * L, D), ctx.reshape(B * LC, D)], axis=0)
    cc = jnp.zeros((MOD_ROWS, D), F32).at[:B].set(c).at[B].set(c_ctx)
    mods = _ada_mods(cc, ada_w, ada_b)

    rw_t = router_w.T.astype(BF16)
    rb = router_bias.reshape(N_EXPERTS, 1).astype(F32)
    rw16 = jnp.zeros((N_EXPERTS, 16, D), BF16).at[:, 0, :].set(rw_t)
    dft = {n: tuple(jnp.asarray(t, dtype=BF16) for t in _dft_tables(n)) for n in sorted({L, LC})}

    resid = (X,)
    for i in range(DEPTH):
        need_ctx = i < DEPTH - 1
        j = i // 2
        mod = mods[i]
        if i % 2 == 0:
            X, zpre = _hy_in(tk, resid, mod, norm_mix_g[i], hy_in_w[j].astype(BF16), hy_in_b[j])
            ys = []
            for (n, blk0) in ((L, 0), (LC, tk.n_lat // LC)):
                cm, sm = dft[n]
                kr, ki, kn = _hy_filter(n, D, hy_f_w1[j], hy_f_b1[j], hy_f_w2[j], hy_f_b2[j], hy_f_w3[j],
                                        hy_sin_freq[j], cm, sm)
                ys.append(_hy_conv(tk, zpre, hy_conv_w[j], hy_conv_b[j].reshape(1, -1),
                                   hy_f_bias[j].reshape(1, D), kr, ki, kn, cm, sm, n, blk0))
            w_out, b_out = hy_out_w[j].astype(BF16), hy_out_b[j]
        else:
            wqu, wkvd, wkvu = _mla_weights(mla_wq_up[j], mla_wkv_down[j], mla_wkv_up[j])
            X, Q, K, V = _mla_proj(tk, resid, mod, norm_mix_g[i], mla_wq_down[j].astype(BF16), mla_q_norm_g[j],
                                   wqu, wkvd, mla_kv_norm_g[j], wkvu, _head_gain_rows(mla_q_head_g[j]),
                                   _head_gain_rows(mla_k_head_g[j]))
            ys = [_attention(tk, Q, K, V, True)]
            if need_ctx:
                ys.append(_attention(tk, Q, K, V, False))
            w_out, b_out = mla_wo[j].astype(BF16), jnp.zeros((D,), F32)

        Xn, h3, route = _post_mixer(tk, ys, w_out, b_out, X, mod, norm_ffn_g[i], rw_t, rb)
        w13 = jnp.concatenate([exp_w1[i], exp_w3[i]], axis=-1).astype(BF16)
        s13 = jnp.concatenate([sh_w1[i], sh_w3[i]], axis=-1).astype(BF16)
        f3 = _moe_layer(h3, route, rw16, w13, exp_w2[i].astype(BF16), s13, sh_w2[i].astype(BF16))
        resid = (Xn, f3, mod)
    return _final_combine(tk, *resid).reshape(B, L, D)
```
